```python
import jax, jax.numpy as jnp
from jax import lax
import numpy as np

D_MODEL = 2048
BATCH = 4
SEQ = 8192
DEPTH = 4

GRID_W = 64
N_BRANCH = 4
BRANCH_WIDTH = D_MODEL // 4
NA_HEADS = 8
NA_HEAD_DIM = BRANCH_WIDTH // NA_HEADS
NA_KH_MAX = 8
NA_KW = 16
NA_KBW = 2 * NA_KW
POOL_WINDOWS = (2, 4, 8, 16)
POOL_GROUPS = len(POOL_WINDOWS)
POOL_GROUP_DIM = BRANCH_WIDTH // POOL_GROUPS
SWA_Q_HEADS = 8
SWA_KV_HEADS = 2
SWA_HEAD_DIM = BRANCH_WIDTH // SWA_Q_HEADS
SWA_WINDOW = 128
SWA_BLOCK = 128
ROPE_THETA = 500000.0
ROPE_DIM = SWA_HEAD_DIM // 4
CONV_K = 3
MEM_LEN = 256
MEM_HEADS = 4
MEM_HEAD_DIM = D_MODEL // 16
D_FF = 4 * D_MODEL
LN_EPS = 1e-5
DEEPNORM_ALPHA = (2 * DEPTH) ** 0.25
DEEPNORM_BETA = (8 * DEPTH) ** -0.25
NA_COLS = 3 * BRANCH_WIDTH
POOL_COLS = BRANCH_WIDTH
SWA_Q_COLS = SWA_Q_HEADS * SWA_HEAD_DIM
SWA_KV_COLS = SWA_KV_HEADS * SWA_HEAD_DIM
SWA_COLS = SWA_Q_COLS + 2 * SWA_KV_COLS
CONV_COLS = 3 * BRANCH_WIDTH
IN_COLS = NA_COLS + POOL_COLS + SWA_COLS + CONV_COLS

kernel_name = 'hybrid_parallel_gated_encoder'


def layer_norm(x, g, b):
    xf = x.astype(jnp.float32)
    mu = xf.mean(-1, keepdims=True)
    var = jnp.square(xf - mu).mean(-1, keepdims=True)
    return ((xf - mu) * lax.rsqrt(var + LN_EPS) * g + b).astype(x.dtype)


def neighbourhood_attention(q, k, v, rpb):
    b, t, h, dh = q.shape
    rows = t // GRID_W
    kh = min(NA_KH_MAX, rows)
    ncb = GRID_W // NA_KW
    scale = dh ** -0.5
    qg = jnp.transpose(q.reshape(b, rows, ncb, NA_KW, h, dh), (1, 0, 4, 2, 3, 5))
    kg = jnp.transpose(k.reshape(b, rows, GRID_W, h, dh), (0, 3, 1, 2, 4))
    vg = jnp.transpose(v.reshape(b, rows, GRID_W, h, dh), (0, 3, 1, 2, 4))
    qcol = np.arange(GRID_W).reshape(ncb, NA_KW)
    win_start = np.clip(qcol - NA_KW // 2, 0, GRID_W - NA_KW)
    blk_start = np.clip(np.arange(ncb) * NA_KW - NA_KW // 2, 0, GRID_W - NA_KBW)
    kcol = blk_start[:, None] + np.arange(NA_KBW)[None, :]
    col_ok = (kcol[:, None, :] >= win_start[:, :, None]) & (kcol[:, None, :] < win_start[:, :, None] + NA_KW)
    dcol = np.clip(kcol[:, None, :] - qcol[:, :, None], -(NA_KW - 1), NA_KW - 1) + NA_KW - 1
    col_bias = rpb[:, :, dcol]
    mask = col_ok[:, :, None, :]

    def row_step(args):
        r, q_row = args
        rs = jnp.clip(r - kh // 2, 0, rows - kh)
        k_blk = lax.dynamic_slice_in_dim(kg, rs, kh, axis=2)[:, :, :, kcol, :]
        v_blk = lax.dynamic_slice_in_dim(vg, rs, kh, axis=2)[:, :, :, kcol, :]
        s = jnp.einsum('bhnqd,bhinkd->bhnqik', q_row, k_blk).astype(jnp.float32) * scale
        drow = rs + jnp.arange(kh) - r + NA_KH_MAX - 1
        bias = jnp.transpose(col_bias[:, drow], (0, 2, 3, 1, 4))
        s = jnp.where(mask, s + bias[None].astype(jnp.float32), -jnp.inf)
        p = jax.nn.softmax(s, axis=(-2, -1))
        return jnp.einsum('bhnqik,bhinkd->bhnqd', p.astype(v.dtype), v_blk)

    out = lax.map(row_step, (jnp.arange(rows), qg))
    return jnp.transpose(out, (1, 0, 3, 4, 2, 5)).reshape(b, t, h * dh)


def multiscale_pool(u, w_grp, scale):
    b, t, c = u.shape
    uf = u.astype(jnp.float32)
    csum = jnp.concatenate([jnp.zeros((b, 1, c), jnp.float32), jnp.cumsum(uf, axis=1)], axis=1)
    pos = jnp.arange(t)
    groups = []
    for g, w in enumerate(POOL_WINDOWS):
        lo = jnp.clip(pos - w // 2, 0, t)
        hi = jnp.clip(pos + w // 2, 0, t)
        cg = csum[:, :, g * POOL_GROUP_DIM:(g + 1) * POOL_GROUP_DIM]
        mean = (cg[:, hi] - cg[:, lo]) / (hi - lo).astype(jnp.float32)[None, :, None]
        groups.append(mean - uf[:, :, g * POOL_GROUP_DIM:(g + 1) * POOL_GROUP_DIM])
    y = jnp.stack(groups, axis=2).astype(u.dtype)
    y = jnp.einsum('btgc,gcd->btgd', y, w_grp).reshape(b, t, c)
    return y * scale


def rotary_partial(x):
    t = x.shape[1]
    half = ROPE_DIM // 2
    inv = ROPE_THETA ** (-jnp.arange(half, dtype=jnp.float32) / half)
    ang = jnp.arange(t, dtype=jnp.float32)[:, None] * inv[None, :]
    cos = jnp.cos(ang)[None, :, None, :]
    sin = jnp.sin(ang)[None, :, None, :]
    xr = x[..., :ROPE_DIM].astype(jnp.float32)
    x1, x2 = xr[..., :half], xr[..., half:]
    rot = jnp.concatenate([x1 * cos - x2 * sin, x2 * cos + x1 * sin], axis=-1)
    return jnp.concatenate([rot.astype(x.dtype), x[..., ROPE_DIM:]], axis=-1)


def windowed_gqa(q, k, v, sinks):
    b, t, hq, dh = q.shape
    hkv = k.shape[2]
    grp = hq // hkv
    nb = t // SWA_BLOCK
    qb = q.reshape(b, nb, SWA_BLOCK, hkv, grp, dh)
    pad = ((0, 0), (SWA_BLOCK, SWA_BLOCK), (0, 0), (0, 0))
    kp = jnp.pad(k, pad).reshape(b, nb + 2, SWA_BLOCK, hkv, dh)
    vp = jnp.pad(v, pad).reshape(b, nb + 2, SWA_BLOCK, hkv, dh)
    kw = jnp.concatenate([kp[:, :-2], kp[:, 1:-1], kp[:, 2:]], axis=2)
    vw = jnp.concatenate([vp[:, :-2], vp[:, 1:-1], vp[:, 2:]], axis=2)
    s = jnp.einsum('bnqhgd,bnkhd->bhgnqk', qb, kw).astype(jnp.float32) * (dh ** -0.5)
    qpos = jnp.arange(nb)[:, None] * SWA_BLOCK + jnp.arange(SWA_BLOCK)[None, :]
    kpos = jnp.arange(nb)[:, None] * SWA_BLOCK - SWA_BLOCK + jnp.arange(3 * SWA_BLOCK)[None, :]
    rel = kpos[:, None, :] - qpos[:, :, None]
    ok = (jnp.abs(rel) <= SWA_WINDOW) & (kpos[:, None, :] >= 0) & (kpos[:, None, :] < t)
    s = jnp.where(ok, s, -jnp.inf)
    sink = sinks.astype(jnp.float32).reshape(hkv, grp)[None, :, :, None, None, None]
    m = jnp.maximum(s.max(-1, keepdims=True), sink)
    p = jnp.exp(s - m)
    p = p / (p.sum(-1, keepdims=True) + jnp.exp(sink - m))
    o = jnp.einsum('bhgnqk,bnkhd->bnqhgd', p.astype(v.dtype), vw)
    return o.reshape(b, t, hq * dh)


def short_gated_conv(h, gate_b, gate_c, w_conv):
    t = h.shape[1]
    z = gate_c * h
    zp = jnp.pad(z, ((0, 0), (CONV_K // 2, CONV_K // 2), (0, 0)))
    y = zp[:, 0:t] * w_conv[0]
    for j in range(1, CONV_K):
        y = y + zp[:, j:j + t] * w_conv[j]
    return gate_b * y


def memory_cross_attention(x, mem, w_q, w_kv, w_o):
    b, t, _ = x.shape
    m = mem.shape[1]
    q = (x @ w_q).reshape(b, t, MEM_HEADS, MEM_HEAD_DIM)
    kv = (mem @ w_kv).reshape(b, m, 2, MEM_HEADS, MEM_HEAD_DIM)
    s = jnp.einsum('bthd,bmhd->bhtm', q, kv[:, :, 0]).astype(jnp.float32) * (MEM_HEAD_DIM ** -0.5)
    p = jax.nn.softmax(s, axis=-1).astype(x.dtype)
    o = jnp.einsum('bhtm,bmhd->bthd', p, kv[:, :, 1]).reshape(b, t, MEM_HEADS * MEM_HEAD_DIM)
    return o @ w_o


def setup_inputs(seed: int = 0) -> dict:
    key = jax.random.key(seed)
    ks = jax.random.split(key, 24)

    def nrm(k, shape, s):
        return jax.random.normal(k, shape, jnp.float32) * s

    L = DEPTH
    return {
        'x': nrm(ks[0], (BATCH, SEQ, D_MODEL), 1.0),
        'mem': nrm(ks[1], (BATCH, MEM_LEN, D_MODEL), 1.0),
        'w_in': nrm(ks[2], (L, D_MODEL, IN_COLS), D_MODEL ** -0.5),
        'na_rpb': nrm(ks[3], (L, NA_HEADS, 2 * NA_KH_MAX - 1, 2 * NA_KW - 1), 0.1),
        'pool_w': nrm(ks[4], (L, POOL_GROUPS, POOL_GROUP_DIM, POOL_GROUP_DIM), POOL_GROUP_DIM ** -0.5),
        'pool_scale': 1.0 + nrm(ks[5], (L, BRANCH_WIDTH), 0.1),
        'swa_sinks': nrm(ks[6], (L, SWA_Q_HEADS), 0.5),
        'conv_w': nrm(ks[7], (L, CONV_K, BRANCH_WIDTH), CONV_K ** -0.5),
        'w_branch': nrm(ks[8], (L, N_BRANCH, BRANCH_WIDTH, D_MODEL), BRANCH_WIDTH ** -0.5),
        'w_gate': nrm(ks[9], (L, N_BRANCH, D_MODEL, D_MODEL), D_MODEL ** -0.5),
        'b_gate': nrm(ks[10], (L, N_BRANCH, D_MODEL), 0.02),
        'w_mix_out': nrm(ks[11], (L, D_MODEL, D_MODEL), D_MODEL ** -0.5 * DEEPNORM_BETA),
        'ln1_g': 1.0 + nrm(ks[12], (L, D_MODEL), 0.02),
        'ln1_b': nrm(ks[13], (L, D_MODEL), 0.02),
        'wq_mem': nrm(ks[14], (L, D_MODEL, MEM_HEADS * MEM_HEAD_DIM), D_MODEL ** -0.5),
        'wkv_mem': nrm(ks[15], (L, D_MODEL, 2 * MEM_HEADS * MEM_HEAD_DIM), D_MODEL ** -0.5),
        'wo_mem': nrm(ks[16], (L, MEM_HEADS * MEM_HEAD_DIM, D_MODEL), (MEM_HEADS * MEM_HEAD_DIM) ** -0.5 * DEEPNORM_BETA),
        'ln2_g': 1.0 + nrm(ks[17], (L, D_MODEL), 0.02),
        'ln2_b': nrm(ks[18], (L, D_MODEL), 0.02),
        'w_ff1': nrm(ks[19], (L, D_MODEL, D_FF), D_MODEL ** -0.5),
        'w_ff2': nrm(ks[20], (L, D_FF, D_MODEL), D_FF ** -0.5 * DEEPNORM_BETA),
        'ln3_g': 1.0 + nrm(ks[21], (L, D_MODEL), 0.02),
        'ln3_b': nrm(ks[22], (L, D_MODEL), 0.02),
    }


def reference(x, mem, w_in, na_rpb, pool_w, pool_scale, swa_sinks, conv_w, w_branch, w_gate, b_gate,
              w_mix_out, ln1_g, ln1_b, wq_mem, wkv_mem, wo_mem, ln2_g, ln2_b, w_ff1, w_ff2, ln3_g, ln3_b):
    b, t, _ = x.shape
    o_pool = NA_COLS
    o_swa = o_pool + POOL_COLS
    o_conv = o_swa + SWA_COLS
    for l in range(DEPTH):
        u = x
        proj = u @ w_in[l]
        qkv_a = proj[..., :NA_COLS].reshape(b, t, 3, NA_HEADS, NA_HEAD_DIM)
        y_a = neighbourhood_attention(qkv_a[:, :, 0], qkv_a[:, :, 1], qkv_a[:, :, 2], na_rpb[l])
        y_b = multiscale_pool(proj[..., o_pool:o_swa], pool_w[l], pool_scale[l])
        sw = proj[..., o_swa:o_conv]
        q_c = rotary_partial(sw[..., :SWA_Q_COLS].reshape(b, t, SWA_Q_HEADS, SWA_HEAD_DIM))
        k_c = rotary_partial(sw[..., SWA_Q_COLS:SWA_Q_COLS + SWA_KV_COLS].reshape(b, t, SWA_KV_HEADS, SWA_HEAD_DIM))
        v_c = sw[..., SWA_Q_COLS + SWA_KV_COLS:].reshape(b, t, SWA_KV_HEADS, SWA_HEAD_DIM)
        y_c = windowed_gqa(q_c, k_c, v_c, swa_sinks[l])
        cv = proj[..., o_conv:]
        y_d = short_gated_conv(cv[..., :BRANCH_WIDTH], cv[..., BRANCH_WIDTH:2 * BRANCH_WIDTH],
                               cv[..., 2 * BRANCH_WIDTH:], conv_w[l])
        mix = jnp.zeros_like(u)
        for i, y in enumerate((y_a, y_b, y_c, y_d)):
            gate = jax.nn.sigmoid(u @ w_gate[l, i] + b_gate[l, i])
            mix = mix + gate * (y @ w_branch[l, i])
        x = layer_norm(DEEPNORM_ALPHA * x + mix @ w_mix_out[l], ln1_g[l], ln1_b[l])
        x = layer_norm(DEEPNORM_ALPHA * x + memory_cross_attention(x, mem, wq_mem[l], wkv_mem[l], wo_mem[l]),
                       ln2_g[l], ln2_b[l])
        ff = jnp.square(jax.nn.relu(x @ w_ff1[l])) @ w_ff2[l]
        x = layer_norm(DEEPNORM_ALPHA * x + ff, ln3_g[l], ln3_b[l])
    return x
```

```python
import functools

import numpy as np
import jax
import jax.numpy as jnp
from jax import lax
from jax.experimental import pallas as pl
from jax.experimental.pallas import tpu as pltpu

F32 = jnp.float32
BF16 = jnp.bfloat16

LANES = 128
V7X_VMEM_LIMIT_BYTES = 56 * 1024 * 1024

GRID_W = 64
NA_HEADS = 8
NA_KH = 8
NA_KW = 16
POOL_WINDOWS = (2, 4, 8, 16)
POOL_HALO = 16
SWA_Q_HEADS = 8
SWA_KV_HEADS = 2
SWA_BLOCK = 128
ROPE_THETA = 500000.0
MEM_HEADS = 4
LN_EPS = 1e-5
NEG_INF = float("-inf")


def _params(n_grid_axes):
    return pltpu.CompilerParams(
        dimension_semantics=("parallel",) * n_grid_axes,
        vmem_limit_bytes=V7X_VMEM_LIMIT_BYTES,
    )


def _resident(shape):
    zeros = (0,) * len(shape)
    return pl.BlockSpec(shape, lambda *_: zeros, pipeline_mode=pl.Buffered(1))


def _dot(a, b):
    return jnp.dot(a, b, preferred_element_type=F32)


def _dot_nt(a, b):
    return lax.dot_general(a, b, (((1,), (1,)), ((), ())), preferred_element_type=F32)


def _layer_norm(z, g, b):
    mu = jnp.mean(z, axis=-1, keepdims=True)
    zc = z - mu
    var = jnp.mean(zc * zc, axis=-1, keepdims=True)
    return zc * lax.rsqrt(var + LN_EPS) * g + b


def _in_proj_kernel(x_ref, w_ref, rope_ref, na_ref, pool_ref, q_ref, kv_ref, conv_ref, *, bw):
    x = x_ref[...]
    na_cols, o_pool, o_swa = 3 * bw, 3 * bw, 4 * bw
    swa_q, swa_kv = bw, bw // 4
    o_conv = o_swa + swa_q + 2 * swa_kv
    for c in range(0, na_cols, bw):
        na_ref[:, c:c + bw] = _dot(x, w_ref[:, c:c + bw]).astype(BF16)
    pool_ref[...] = _dot(x, w_ref[:, o_pool:o_swa])
    for c in range(0, 3 * bw, bw):
        conv_ref[:, c:c + bw] = _dot(x, w_ref[:, o_conv + c:o_conv + c + bw])

    cos, sin_lo, sin_hi = rope_ref[0], rope_ref[1], rope_ref[2]

    def rotate(v):
        return (v * cos + pltpu.roll(v, 8, 1) * sin_hi + pltpu.roll(v, LANES - 8, 1) * sin_lo)

    for c in range(0, swa_q, LANES):
        qc = _dot(x, w_ref[:, o_swa + c:o_swa + c + LANES])
        q_ref[:, c:c + LANES] = rotate(qc).astype(BF16)
    o_k = o_swa + swa_q
    kv_ref[:, :swa_kv] = rotate(_dot(x, w_ref[:, o_k:o_k + swa_kv])).astype(BF16)
    kv_ref[:, swa_kv:] = _dot(x, w_ref[:, o_k + swa_kv:o_k + 2 * swa_kv]).astype(BF16)


def _in_proj(xb, w, rope, seq, tm):
    n, d = xb.shape
    bw = d // 4
    in_cols = w.shape[1]
    t_tiles = seq // tm
    row = lambda c: pl.BlockSpec((tm, c), lambda i: (i, 0))
    return pl.pallas_call(
        functools.partial(_in_proj_kernel, bw=bw),
        grid=(n // tm,),
        in_specs=[
            row(d),
            _resident((d, in_cols)),
            pl.BlockSpec((3, tm, LANES), lambda i: (0, i % t_tiles, 0)),
        ],
        out_specs=[row(3 * bw), row(bw), row(bw), row(bw // 2), row(3 * bw)],
        out_shape=[
            jax.ShapeDtypeStruct((n, 3 * bw), BF16),
            jax.ShapeDtypeStruct((n, bw), F32),
            jax.ShapeDtypeStruct((n, bw), BF16),
            jax.ShapeDtypeStruct((n, bw // 2), BF16),
            jax.ShapeDtypeStruct((n, 3 * bw), F32),
        ],
        compiler_params=_params(1),
        name="in_proj",
    )(xb, w, rope)


def _rope_tables(seq):
    half = 8
    inv = ROPE_THETA ** (-jnp.arange(half, dtype=F32) / half)
    ang = jnp.arange(seq, dtype=F32)[:, None] * inv[None, :]
    cos, sin = jnp.cos(ang), jnp.sin(ang)
    one = jnp.ones((seq, 64 - 2 * half), F32)
    zero = jnp.zeros((seq, 64 - 2 * half), F32)
    z8 = jnp.zeros((seq, half), F32)
    c64 = jnp.concatenate([cos, cos, one], axis=1)
    lo64 = jnp.concatenate([-sin, z8, zero], axis=1)
    hi64 = jnp.concatenate([z8, sin, zero], axis=1)
    tile2 = lambda a: jnp.concatenate([a, a], axis=1)
    return jnp.stack([tile2(c64), tile2(lo64), tile2(hi64)])


def _na_kernel(q_ref, kp_ref, kc_ref, kn_ref, vp_ref, vc_ref, vn_ref, bias_ref, o_ref,
               kbuf, vbuf, *, rows, heads):
    blk = NA_KH * GRID_W
    j = pl.program_id(1)
    kbuf[0:blk] = kp_ref[...]
    kbuf[blk:2 * blk] = kc_ref[...]
    kbuf[2 * blk:] = kn_ref[...]
    vbuf[0:blk] = vp_ref[...]
    vbuf[blk:2 * blk] = vc_ref[...]
    vbuf[2 * blk:] = vn_ref[...]
    lane = lax.broadcasted_iota(jnp.int32, (GRID_W, LANES), 1)
    low = lane < 64
    scale = 64 ** -0.5

    def row_body(rr, carry):
        r = j * NA_KH + rr
        rs = jnp.clip(r - NA_KH // 2, 0, rows - NA_KH)
        delta = r - rs
        start = pl.multiple_of((rs - j * NA_KH + NA_KH) * GRID_W, GRID_W)
        q0 = pl.multiple_of(rr * GRID_W, GRID_W)
        for p in range(heads // 2):
            cols = slice(p * LANES, (p + 1) * LANES)
            qs = q_ref[pl.ds(q0, GRID_W), cols]
            kw = kbuf[pl.ds(start, blk), cols]
            vw = vbuf[pl.ds(start, blk), cols]
            outs = []
            for hh in range(2):
                keep = low if hh == 0 else jnp.logical_not(low)
                qh = jnp.where(keep, qs, jnp.zeros_like(qs))
                s = _dot_nt(qh, kw) * scale + bias_ref[delta, 2 * p + hh]
                m = jnp.max(s, axis=-1, keepdims=True)
                e = jnp.exp(s - m)
                l = jnp.sum(e, axis=-1, keepdims=True)
                outs.append(_dot(e.astype(BF16), vw) / l)
            o_ref[pl.ds(q0, GRID_W), cols] = jnp.where(low, outs[0], outs[1]).astype(BF16)
        return carry

    lax.fori_loop(0, NA_KH, row_body, 0)


def _na_bias_table(rpb):
    qc = np.arange(GRID_W)
    kc = np.arange(GRID_W)
    ws = np.clip(qc - NA_KW // 2, 0, GRID_W - NA_KW)
    ok = (kc[None, :] >= ws[:, None]) & (kc[None, :] < ws[:, None] + NA_KW)
    dcol = np.clip(kc[None, :] - qc[:, None], -(NA_KW - 1), NA_KW - 1) + NA_KW - 1
    u = jnp.where(ok[None, None], rpb[:, :, dcol].astype(F32), NEG_INF)
    drow = np.arange(NA_KH)[None, :] - np.arange(NA_KH)[:, None] + NA_KH - 1
    t = u[:, drow]
    t = jnp.transpose(t, (1, 0, 3, 2, 4))
    return t.reshape(NA_KH, rpb.shape[0], GRID_W, NA_KH * GRID_W)


def _neighbourhood_attention(na, bias, batch, seq):
    n, c3 = na.shape
    bw = c3 // 3
    rows = seq // GRID_W
    blk = NA_KH * GRID_W
    nblk = seq // blk
    na3 = na.reshape(batch, seq, c3)

    def spec(col, shift):
        def idx(b, j):
            return (b, jnp.clip(j + shift, 0, nblk - 1), col)
        return pl.BlockSpec((None, blk, bw), idx)

    out = pl.pallas_call(
        functools.partial(_na_kernel, rows=rows, heads=NA_HEADS),
        grid=(batch, nblk),
        in_specs=[spec(0, 0), spec(1, -1), spec(1, 0), spec(1, 1), spec(2, -1), spec(2, 0), spec(2, 1),
                  _resident(bias.shape)],
        out_specs=pl.BlockSpec((None, blk, bw), lambda b, j: (b, j, 0)),
        out_shape=jax.ShapeDtypeStruct((batch, seq, bw), BF16),
        scratch_shapes=[pltpu.VMEM((3 * blk, bw), BF16), pltpu.VMEM((3 * blk, bw), BF16)],
        compiler_params=_params(2),
        name="na_attn",
    )(na3, na3, na3, na3, na3, na3, na3, bias)
    return out.reshape(n, bw)


def _pool_kernel(cur_ref, prev_ref, next_ref, w_ref, scale_ref, o_ref, *, seq, tm):
    i = pl.program_id(1)
    nt = pl.num_programs(1)
    prev = jnp.where(i > 0, prev_ref[...], 0.0)
    nxt = jnp.where(i < nt - 1, next_ref[...], 0.0)
    ext = jnp.concatenate([prev, cur_ref[...], nxt], axis=0)
    length = tm + 2 * POOL_HALO
    pos = i * tm - POOL_HALO + lax.broadcasted_iota(jnp.int32, (length, 1), 0)
    gd = LANES
    for g, win in enumerate(POOL_WINDOWS):
        xg = ext[:, g * gd:(g + 1) * gd]
        s = xg + pltpu.roll(xg, 1, 0)
        step = 1
        while 2 * step < win:
            s = pltpu.roll(s, step, 0) + pltpu.roll(s, length - step, 0)
            step *= 2
        hi = jnp.minimum(pos + win // 2, seq)
        lo = jnp.maximum(pos - win // 2, 0)
        cnt = jnp.maximum(hi - lo, 1).astype(F32)
        diff = (s / cnt - xg)[POOL_HALO:POOL_HALO + tm]
        y = _dot(diff.astype(BF16), w_ref[g]) * scale_ref[:, g * gd:(g + 1) * gd]
        o_ref[:, g * gd:(g + 1) * gd] = y.astype(BF16)


def _multiscale_pool(u, w_grp, scale, batch, seq, tm):
    n, c = u.shape
    u3 = u.reshape(batch, seq, c)
    hb = tm // POOL_HALO
    last = seq // POOL_HALO - 1
    out = pl.pallas_call(
        functools.partial(_pool_kernel, seq=seq, tm=tm),
        grid=(batch, seq // tm),
        in_specs=[
            pl.BlockSpec((None, tm, c), lambda b, i: (b, i, 0)),
            pl.BlockSpec((None, POOL_HALO, c), lambda b, i: (b, jnp.maximum(i * hb - 1, 0), 0)),
            pl.BlockSpec((None, POOL_HALO, c), lambda b, i: (b, jnp.minimum((i + 1) * hb, last), 0)),
            _resident(w_grp.shape),
            _resident(scale.shape),
        ],
        out_specs=pl.BlockSpec((None, tm, c), lambda b, i: (b, i, 0)),
        out_shape=jax.ShapeDtypeStruct((batch, seq, c), BF16),
        compiler_params=_params(2),
        name="pool_mixer",
    )(u3, u3, u3, w_grp, scale)
    return out.reshape(n, c)


def _swa_kernel(q_ref, kvp_ref, kvc_ref, kvn_ref, mask_ref, sink_ref, o_ref):
    blk = SWA_BLOCK
    n_slab = q_ref.shape[1] // LANES
    lane = lax.broadcasted_iota(jnp.int32, (blk, LANES), 1)
    low = lane < 64
    k = jnp.concatenate([kvp_ref[:, :LANES], kvc_ref[:, :LANES], kvn_ref[:, :LANES]], axis=0)
    v = jnp.concatenate([kvp_ref[:, LANES:], kvc_ref[:, LANES:], kvn_ref[:, LANES:]], axis=0)
    mask = mask_ref[...]
    scale = 64 ** -0.5
    outs = []
    for g in range(SWA_KV_HEADS):
        keep = low if g == 0 else jnp.logical_not(low)
        slabs = []
        for jj in range(n_slab):
            qs = q_ref[:, jj * LANES:(jj + 1) * LANES]
            slabs.append(jnp.where(keep, qs, jnp.zeros_like(qs)))
        q4 = jnp.concatenate(slabs, axis=0)
        s = _dot_nt(q4, k) * scale + mask
        sink = sink_ref[g]
        m = jnp.maximum(jnp.max(s, axis=-1, keepdims=True), sink)
        e = jnp.exp(s - m)
        denom = jnp.sum(e, axis=-1, keepdims=True) + jnp.exp(sink - m)
        outs.append(_dot(e.astype(BF16), v) / denom)
    for jj in range(n_slab):
        rows_ = slice(jj * blk, (jj + 1) * blk)
        o_ref[:, jj * LANES:(jj + 1) * LANES] = jnp.where(low, outs[0][rows_], outs[1][rows_]).astype(BF16)


def _swa_mask_table():
    blk = SWA_BLOCK
    r = np.arange(4 * blk)[:, None] % blk
    c = np.arange(3 * blk)[None, :]
    band = np.abs(c - blk - r) <= blk
    variants = []
    for last in (False, True):
        for first in (False, True):
            ok = band.copy()
            if first:
                ok &= c >= blk
            if last:
                ok &= c < 2 * blk
            variants.append(np.where(ok, 0.0, NEG_INF).astype(np.float32))
    order = [variants[0], variants[1], variants[2], variants[3]]
    return jnp.asarray(np.stack(order))


def _windowed_gqa(q, kv, mask, sink_cols, batch, seq):
    n, qc = q.shape
    kvc = kv.shape[1]
    blk = SWA_BLOCK
    nb = seq // blk
    q3 = q.reshape(batch, seq, qc)
    kv3 = kv.reshape(batch, seq, kvc)

    def kv_spec(shift):
        return pl.BlockSpec((None, blk, kvc), lambda b, i: (b, jnp.clip(i + shift, 0, nb - 1), 0))

    def mask_idx(b, i):
        return ((i == 0).astype(jnp.int32) + 2 * (i == nb - 1).astype(jnp.int32), 0, 0)

    out = pl.pallas_call(
        _swa_kernel,
        grid=(batch, nb),
        in_specs=[
            pl.BlockSpec((None, blk, qc), lambda b, i: (b, i, 0)),
            kv_spec(-1), kv_spec(0), kv_spec(1),
            pl.BlockSpec((None,) + mask.shape[1:], mask_idx),
            _resident(sink_cols.shape),
        ],
        out_specs=pl.BlockSpec((None, blk, qc), lambda b, i: (b, i, 0)),
        out_shape=jax.ShapeDtypeStruct((batch, seq, qc), BF16),
        compiler_params=_params(2),
        name="swa_attn",
    )(q3, kv3, kv3, kv3, mask, sink_cols)
    return out.reshape(n, qc)


def _conv_kernel(cur_ref, prev_ref, next_ref, w_ref, o_ref, *, tm, bw):
    i = pl.program_id(1)
    nt = pl.num_programs(1)
    h, gate_b, gate_c = cur_ref[:, :bw], cur_ref[:, bw:2 * bw], cur_ref[:, 2 * bw:]
    z = gate_c * h
    z_before = jnp.where(i > 0, prev_ref[7:8, 2 * bw:] * prev_ref[7:8, :bw], 0.0)
    z_after = jnp.where(i < nt - 1, next_ref[0:1, 2 * bw:] * next_ref[0:1, :bw], 0.0)
    row = lax.broadcasted_iota(jnp.int32, (tm, 1), 0)
    z_prev = jnp.where(row == 0, z_before, pltpu.roll(z, 1, 0))
    z_next = jnp.where(row == tm - 1, z_after, pltpu.roll(z, tm - 1, 0))
    y = z_prev * w_ref[0:1, :] + z * w_ref[1:2, :] + z_next * w_ref[2:3, :]
    o_ref[...] = (gate_b * y).astype(BF16)


def _short_gated_conv(cv, w_conv, batch, seq, tm):
    n, c3 = cv.shape
    bw = c3 // 3
    cv3 = cv.reshape(batch, seq, c3)
    hb = tm // 8
    last = seq // 8 - 1
    out = pl.pallas_call(
        functools.partial(_conv_kernel, tm=tm, bw=bw),
        grid=(batch, seq // tm),
        in_specs=[
            pl.BlockSpec((None, tm, c3), lambda b, i: (b, i, 0)),
            pl.BlockSpec((None, 8, c3), lambda b, i: (b, jnp.maximum(i * hb - 1, 0), 0)),
            pl.BlockSpec((None, 8, c3), lambda b, i: (b, jnp.minimum((i + 1) * hb, last), 0)),
            _resident(w_conv.shape),
        ],
        out_specs=pl.BlockSpec((None, tm, bw), lambda b, i: (b, i, 0)),
        out_shape=jax.ShapeDtypeStruct((batch, seq, bw), BF16),
        compiler_params=_params(2),
        name="conv_mixer",
    )(cv3, cv3, cv3, w_conv)
    return out.reshape(n, bw)


def _merge_kernel(u_ref, ya_ref, yb_ref, yc_ref, yd_ref, wg_ref, bg_ref, wb_ref, o_ref):
    u = u_ref[...]
    acc = None
    for br, y_ref in enumerate((ya_ref, yb_ref, yc_ref, yd_ref)):
        gate = jax.nn.sigmoid(_dot(u, wg_ref[br]) + bg_ref[br])
        term = gate * _dot(y_ref[...], wb_ref[br])
        acc = term if acc is None else acc + term
    o_ref[...] = acc.astype(BF16)


def _gated_merge(u, ys, wg, bg, wb, tm, tn):
    n, d = u.shape
    bw = ys[0].shape[1]
    nbr = wg.shape[0]
    y_spec = pl.BlockSpec((tm, bw), lambda i, j: (i, 0))
    return pl.pallas_call(
        _merge_kernel,
        grid=(n // tm, d // tn),
        in_specs=[
            pl.BlockSpec((tm, d), lambda i, j: (i, 0)),
            y_spec, y_spec, y_spec, y_spec,
            pl.BlockSpec((nbr, d, tn), lambda i, j: (0, 0, j)),
            pl.BlockSpec((nbr, 1, tn), lambda i, j: (0, 0, j)),
            pl.BlockSpec((nbr, bw, tn), lambda i, j: (0, 0, j)),
        ],
        out_specs=pl.BlockSpec((tm, tn), lambda i, j: (i, j)),
        out_shape=jax.ShapeDtypeStruct((n, d), BF16),
        compiler_params=_params(2),
        name="gated_merge",
    )(u, *ys, wg, bg, wb)


def _proj_ln_kernel(a_ref, x_ref, w_ref, g_ref, b_ref, of_ref, ob_ref, *, alpha):
    z = alpha * x_ref[...] + _dot(a_ref[...], w_ref[...])
    y = _layer_norm(z, g_ref[...], b_ref[...])
    of_ref[...] = y
    ob_ref[...] = y.astype(BF16)


def _proj_ln(a, x, w, g, b, alpha, tm):
    n, d = x.shape
    k = a.shape[1]
    row = lambda c: pl.BlockSpec((tm, c), lambda i: (i, 0))
    return pl.pallas_call(
        functools.partial(_proj_ln_kernel, alpha=alpha),
        grid=(n // tm,),
        in_specs=[row(k), row(d), _resident((k, d)), _resident((1, d)), _resident((1, d))],
        out_specs=[row(d), row(d)],
        out_shape=[jax.ShapeDtypeStruct((n, d), F32), jax.ShapeDtypeStruct((n, d), BF16)],
        compiler_params=_params(1),
        name="proj_ln",
    )(a, x, w, g, b)


def _matmul_kernel(a_ref, w_ref, o_ref):
    o_ref[...] = _dot(a_ref[...], w_ref[...]).astype(o_ref.dtype)


def _matmul(a, w, tm, out_dtype):
    n, k = a.shape
    m = w.shape[1]
    return pl.pallas_call(
        _matmul_kernel,
        grid=(n // tm,),
        in_specs=[pl.BlockSpec((tm, k), lambda i: (i, 0)), _resident((k, m))],
        out_specs=pl.BlockSpec((tm, m), lambda i: (i, 0)),
        out_shape=jax.ShapeDtypeStruct((n, m), out_dtype),
        compiler_params=_params(1),
        name="matmul",
    )(a, w)


def _mem_attn_kernel(xb_ref, x_ref, wq_ref, kv_ref, wo_ref, g_ref, b_ref, of_ref, ob_ref, *, alpha):
    hd = wq_ref.shape[1] // MEM_HEADS
    kcols = MEM_HEADS * hd
    q = _dot(xb_ref[...], wq_ref[...]).astype(BF16)
    scale = hd ** -0.5
    heads = []
    for h in range(MEM_HEADS):
        kh = kv_ref[:, h * hd:(h + 1) * hd]
        vh = kv_ref[:, kcols + h * hd:kcols + (h + 1) * hd]
        s = _dot_nt(q[:, h * hd:(h + 1) * hd], kh) * scale
        m = jnp.max(s, axis=-1, keepdims=True)
        e = jnp.exp(s - m)
        l = jnp.sum(e, axis=-1, keepdims=True)
        heads.append((_dot(e.astype(BF16), vh) / l).astype(BF16))
    o = jnp.concatenate(heads, axis=-1)
    z = alpha * x_ref[...] + _dot(o, wo_ref[...])
    y = _layer_norm(z, g_ref[...], b_ref[...])
    of_ref[...] = y
    ob_ref[...] = y.astype(BF16)


def _mem_attention(xb, x, wq, kv, wo, g, b, alpha, batch, seq, tm):
    n, d = x.shape
    mlen = kv.shape[0] // batch
    kvw = kv.shape[1]
    t_tiles = seq // tm
    row = lambda c: pl.BlockSpec((tm, c), lambda i: (i, 0))
    return pl.pallas_call(
        functools.partial(_mem_attn_kernel, alpha=alpha),
        grid=(n // tm,),
        in_specs=[
            row(d), row(d), _resident(wq.shape),
            pl.BlockSpec((mlen, kvw), lambda i: (i // t_tiles, 0)),
            _resident(wo.shape), _resident((1, d)), _resident((1, d)),
        ],
        out_specs=[row(d), row(d)],
        out_shape=[jax.ShapeDtypeStruct((n, d), F32), jax.ShapeDtypeStruct((n, d), BF16)],
        compiler_params=_params(1),
        name="mem_attn",
    )(xb, x, wq, kv, wo, g, b)


def _ffn_kernel(xb_ref, x_ref, w1_ref, w2_ref, g_ref, b_ref, of_ref, ob_ref, acc_ref, *, alpha):
    f = pl.program_id(1)
    h = jnp.maximum(_dot(xb_ref[...], w1_ref[...]), 0.0)
    part = _dot((h * h).astype(BF16), w2_ref[...])

    @pl.when(f == 0)
    def _():
        acc_ref[...] = part

    @pl.when(f > 0)
    def _():
        acc_ref[...] += part

    @pl.when(f == pl.num_programs(1) - 1)
    def _():
        y = _layer_norm(alpha * x_ref[...] + acc_ref[...], g_ref[...], b_ref[...])
        of_ref[...] = y
        ob_ref[...] = y.astype(BF16)


def _ffn(xb, x, w1, w2, g, b, alpha, tm, tf):
    n, d = x.shape
    dff = w1.shape[1]
    row = lambda c: pl.BlockSpec((tm, c), lambda i, f: (i, 0))
    return pl.pallas_call(
        functools.partial(_ffn_kernel, alpha=alpha),
        grid=(n // tm, dff // tf),
        in_specs=[
            row(d), row(d),
            pl.BlockSpec((d, tf), lambda i, f: (0, f)),
            pl.BlockSpec((tf, d), lambda i, f: (f, 0)),
            _resident((1, d)), _resident((1, d)),
        ],
        out_specs=[row(d), row(d)],
        out_shape=[jax.ShapeDtypeStruct((n, d), F32), jax.ShapeDtypeStruct((n, d), BF16)],
        scratch_shapes=[pltpu.VMEM((tm, d), F32)],
        compiler_params=pltpu.CompilerParams(
            dimension_semantics=("parallel", "arbitrary"),
            vmem_limit_bytes=V7X_VMEM_LIMIT_BYTES,
        ),
        name="ffn",
    )(xb, x, w1, w2, g, b)


def _tile(total, want):
    t = min(total, want)
    assert total % t == 0, (total, want)
    return t


def kernel(x, mem, w_in, na_rpb, pool_w, pool_scale, swa_sinks, conv_w, w_branch, w_gate, b_gate, w_mix_out,
           ln1_g, ln1_b, wq_mem, wkv_mem, wo_mem, ln2_g, ln2_b, w_ff1, w_ff2, ln3_g, ln3_b):
    batch, seq, d = x.shape
    depth = w_in.shape[0]
    n = batch * seq
    bw = d // 4
    mlen = mem.shape[1]
    assert bw == NA_HEADS * 64 == SWA_Q_HEADS * 64 == len(POOL_WINDOWS) * LANES
    assert seq % (NA_KH * GRID_W) == 0 and seq // GRID_W >= NA_KH and seq % SWA_BLOCK == 0
    alpha = (2 * depth) ** 0.25

    tm = _tile(seq, 512)
    tm_merge = _tile(n, 512)
    tn_merge = _tile(d, 512)
    tf = _tile(w_ff1.shape[2], 512)

    o_swa = 4 * bw
    head_order = np.array([0, 4, 1, 5, 2, 6, 3, 7])
    q_cols = (head_order[:, None] * 64 + np.arange(64)[None, :]).reshape(-1)
    col_perm = np.arange(w_in.shape[2])
    col_perm[o_swa:o_swa + bw] = o_swa + q_cols
    w_in_b = w_in[:, :, col_perm].astype(BF16)
    w_branch_b = w_branch.at[:, 2].set(w_branch[:, 2][:, q_cols, :]).astype(BF16)
    w_gate_b = w_gate.astype(BF16)
    b_gate_r = b_gate.reshape(depth, b_gate.shape[1], 1, d)
    w_mix_b = w_mix_out.astype(BF16)
    wq_b, wkv_b, wo_b = wq_mem.astype(BF16), wkv_mem.astype(BF16), wo_mem.astype(BF16)
    w1_b, w2_b = w_ff1.astype(BF16), w_ff2.astype(BF16)
    pool_w_b = pool_w.astype(BF16)
    sink_cols = jnp.repeat(swa_sinks.astype(F32).reshape(depth, SWA_KV_HEADS, 4), SWA_BLOCK, axis=2)[..., None]
    vec = lambda a, l: a[l].reshape(1, d)

    rope = _rope_tables(seq)
    swa_mask = _swa_mask_table()
    mem_b = mem.reshape(batch * mlen, d).astype(BF16)

    xf = x.reshape(n, d)
    xb = xf.astype(BF16)
    for l in range(depth):
        na, pool_u, q_c, kv_c, cv = _in_proj(xb, w_in_b[l], rope, seq, tm)
        y_a = _neighbourhood_attention(na, _na_bias_table(na_rpb[l]), batch, seq)
        y_b = _multiscale_pool(pool_u, pool_w_b[l], pool_scale[l].reshape(1, bw), batch, seq, tm)
        y_c = _windowed_gqa(q_c, kv_c, swa_mask, sink_cols[l], batch, seq)
        y_d = _short_gated_conv(cv, conv_w[l], batch, seq, tm)
        mix = _gated_merge(xb, (y_a, y_b, y_c, y_d), w_gate_b[l], b_gate_r[l], w_branch_b[l], tm_merge, tn_merge)
        xf, xb = _proj_ln(mix, xf, w_mix_b[l], vec(ln1_g, l), vec(ln1_b, l), alpha, tm)
        kv_m = _matmul(mem_b, wkv_b[l], _tile(batch * mlen, 256), BF16)
        xf, xb = _mem_attention(xb, xf, wq_b[l], kv_m, wo_b[l], vec(ln2_g, l), vec(ln2_b, l), alpha,
                                batch, seq, tm)
        xf, xb = _ffn(xb, xf, w1_b[l], w2_b[l], vec(ln3_g, l), vec(ln3_b, l), alpha, tm_merge, tf)
    return xf.reshape(batch, seq, d)
```
